```python
import math
import jax, jax.numpy as jnp
from jax import lax
import numpy as np

D_MODEL = 1024
BATCH = 8
SEQ = 2048
DEPTH = 1

HEAD_DIM = 64
N_HEADS = D_MODEL // HEAD_DIM
MOBA_HEADS = N_HEADS // 2
DIL_HEADS = N_HEADS - MOBA_HEADS
MOBA_WIDTH = MOBA_HEADS * HEAD_DIM
DIL_WIDTH = DIL_HEADS * HEAD_DIM
MIX_WIDTH = MOBA_WIDTH + DIL_WIDTH
IN_WIDTH = 3 * MIX_WIDTH
MOBA_BLOCK = 256
MOBA_TOPK = 3
MOBA_Q_CHUNK = 32
DIL_PAIRS = ((128, 1), (512, 4), (2048, 16))
DIL_BLOCK = 128
REL_BUCKETS = 32
REL_MAX_EXACT = 16
REL_MAX_DIST = 2048
PEER_HEADS = 8
PEER_N_KEYS = 128
PEER_N_EXPERTS = PEER_N_KEYS * PEER_N_KEYS
PEER_TOPK = 16
PEER_DK = 128
PEER_CHUNK = 128
DEEPNORM_ALPHA = (2 * DEPTH) ** 0.25
DEEPNORM_BETA = (8 * DEPTH) ** -0.25
LN_EPS = 1e-5
NEG_INF = -1e30

kernel_name = "hybrid_moba_dilated_peer_block"


def layer_norm(x, g, b):
    xf = x.astype(jnp.float32)
    mu = xf.mean(-1, keepdims=True)
    var = jnp.square(xf - mu).mean(-1, keepdims=True)
    return ((xf - mu) * lax.rsqrt(var + LN_EPS) * g + b).astype(x.dtype)


def rms_norm(x, g):
    xf = x.astype(jnp.float32)
    return (xf * lax.rsqrt(jnp.square(xf).mean(-1, keepdims=True) + LN_EPS) * g).astype(x.dtype)


def t5_bucket(dist):
    n = jnp.maximum(dist, 0)
    nf = jnp.maximum(n, 1).astype(jnp.float32)
    large = REL_MAX_EXACT + (jnp.log(nf / REL_MAX_EXACT) / math.log(REL_MAX_DIST / REL_MAX_EXACT)
                             * (REL_BUCKETS - REL_MAX_EXACT)).astype(jnp.int32)
    large = jnp.minimum(large, REL_BUCKETS - 1)
    return jnp.where(n < REL_MAX_EXACT, n, large)


def moba_attention(q, k, v, bias_tab):
    B, H, S, hd = q.shape
    nb = -(-S // MOBA_BLOCK)
    Sp = nb * MOBA_BLOCK
    pad = ((0, 0), (0, 0), (0, Sp - S), (0, 0))
    kp, vp = jnp.pad(k, pad), jnp.pad(v, pad)
    kblk = kp.reshape(B, H, nb, MOBA_BLOCK, hd)
    vblk = vp.reshape(B, H, nb, MOBA_BLOCK, hd)
    kmean = kblk.astype(jnp.float32).mean(3)
    qblk = jnp.arange(S) // MOBA_BLOCK
    gate = jnp.einsum('bhsd,bhnd->bhsn', q.astype(jnp.float32), kmean)
    past = jnp.arange(nb)[None, :] < qblk[:, None]
    gate = jnp.where(past, gate, -jnp.inf)
    k_sel = min(MOBA_TOPK, nb)
    _, sel = lax.top_k(gate, k_sel)
    sel_valid = jnp.arange(k_sel)[None, :] < qblk[:, None]
    scale = hd ** -0.5
    bi = jnp.arange(B)[:, None, None, None]
    hi = jnp.arange(H)[None, :, None, None]
    hi5 = jnp.arange(H)[None, :, None, None, None]

    def chunk(ci):
        t0 = ci * MOBA_Q_CHUNK
        qc = lax.dynamic_slice_in_dim(q, t0, MOBA_Q_CHUNK, 2)
        selc = lax.dynamic_slice_in_dim(sel, t0, MOBA_Q_CHUNK, 2)
        validc = lax.dynamic_slice_in_dim(sel_valid, t0, MOBA_Q_CHUNK, 0)
        qpos = t0 + jnp.arange(MOBA_Q_CHUNK)
        ks = kblk[bi, hi, selc]
        vs = vblk[bi, hi, selc]
        l_sel = jnp.einsum('bhqd,bhqkjd->bhqkj', qc, ks).astype(jnp.float32) * scale
        kpos_sel = selc[..., None] * MOBA_BLOCK + jnp.arange(MOBA_BLOCK)
        l_sel = l_sel + bias_tab[hi5, t5_bucket(qpos[:, None, None] - kpos_sel)]
        l_sel = jnp.where(validc[None, None, :, :, None], l_sel, NEG_INF)
        own0 = (t0 // MOBA_BLOCK) * MOBA_BLOCK
        ko = lax.dynamic_slice_in_dim(kp, own0, MOBA_BLOCK, 2)
        vo = lax.dynamic_slice_in_dim(vp, own0, MOBA_BLOCK, 2)
        dist = qpos[:, None] - (own0 + jnp.arange(MOBA_BLOCK))[None, :]
        l_own = jnp.einsum('bhqd,bhjd->bhqj', qc, ko).astype(jnp.float32) * scale
        l_own = jnp.where(dist >= 0, l_own + bias_tab[:, t5_bucket(dist)], NEG_INF)
        n_sel = k_sel * MOBA_BLOCK
        p = jax.nn.softmax(jnp.concatenate(
            [l_sel.reshape(B, H, MOBA_Q_CHUNK, n_sel), l_own], -1), axis=-1).astype(v.dtype)
        p_sel = p[..., :n_sel].reshape(B, H, MOBA_Q_CHUNK, k_sel, MOBA_BLOCK)
        return (jnp.einsum('bhqkj,bhqkjd->bhqd', p_sel, vs)
                + jnp.einsum('bhqj,bhjd->bhqd', p[..., n_sel:], vo))

    outs = lax.map(chunk, jnp.arange(S // MOBA_Q_CHUNK))
    return outs.transpose(1, 2, 0, 3, 4).reshape(B, H, S, hd)


def dilated_branch(q, k, v, bias_tab, window, dil):
    B, H, S, hd = q.shape
    L = S // dil
    nblk = -(-L // DIL_BLOCK)
    Lp = nblk * DIL_BLOCK

    def to_blocks(a):
        a = a.reshape(B, H, L, dil, hd).transpose(0, 1, 3, 2, 4)
        a = jnp.pad(a, ((0, 0), (0, 0), (0, 0), (0, Lp - L), (0, 0)))
        return a.reshape(B, H, dil, nblk, DIL_BLOCK, hd)

    qb, kb, vb = to_blocks(q), to_blocks(k), to_blocks(v)
    pad_prev = ((0, 0), (0, 0), (0, 0), (1, 0), (0, 0), (0, 0))
    k2 = jnp.concatenate([jnp.pad(kb, pad_prev)[:, :, :, :-1], kb], axis=4)
    v2 = jnp.concatenate([jnp.pad(vb, pad_prev)[:, :, :, :-1], vb], axis=4)
    logits = jnp.einsum('bhrnqd,bhrnkd->bhrnqk', qb, k2).astype(jnp.float32) * hd ** -0.5
    i = jnp.arange(DIL_BLOCK)[:, None]
    j = jnp.arange(2 * DIL_BLOCK)[None, :]
    off = DIL_BLOCK + i - j
    band = (off >= 0) & (off <= window // dil)
    has_prev = (jnp.arange(nblk)[:, None, None] > 0) | (j[None] >= DIL_BLOCK)
    mask = band[None] & has_prev
    bias = bias_tab[:, t5_bucket(off * dil)]
    logits = jnp.where(mask, logits + bias[:, None, None], NEG_INF)
    lse = jax.nn.logsumexp(logits, axis=-1)
    p = jnp.exp(logits - lse[..., None]).astype(v.dtype)
    o = jnp.einsum('bhrnqk,bhrnkd->bhrnqd', p, v2)
    o = o.reshape(B, H, dil, Lp, hd)[:, :, :, :L].transpose(0, 1, 3, 2, 4).reshape(B, H, S, hd)
    lse = lse.reshape(B, H, dil, Lp)[..., :L].transpose(0, 1, 3, 2).reshape(B, H, S)
    return o, lse


def dilated_attention(q, k, v, bias_tab):
    outs, lses = [], []
    for window, dil in DIL_PAIRS:
        o, l = dilated_branch(q, k, v, bias_tab, window, dil)
        outs.append(o)
        lses.append(l)
    wts = jax.nn.softmax(jnp.stack(lses, 0), axis=0)
    out = jnp.einsum('nbhs,nbhsd->bhsd', wts, jnp.stack(outs, 0).astype(jnp.float32))
    return out.astype(q.dtype)


def peer_ffn(h, w_q, sub_keys, u_tab, v_tab):
    B, S, D = h.shape
    T = B * S
    xt = h.reshape(T, D)
    q = (xt @ w_q).reshape(T, PEER_HEADS, PEER_DK)
    half = PEER_DK // 2
    s1 = jnp.einsum('thd,nd->thn', q[..., :half], sub_keys[0]).astype(jnp.float32)
    s2 = jnp.einsum('thd,nd->thn', q[..., half:], sub_keys[1]).astype(jnp.float32)
    v1, i1 = lax.top_k(s1, PEER_TOPK)
    v2, i2 = lax.top_k(s2, PEER_TOPK)
    cand = (v1[..., :, None] + v2[..., None, :]).reshape(T, PEER_HEADS, PEER_TOPK * PEER_TOPK)
    vals, pos = lax.top_k(cand, PEER_TOPK)
    e = (jnp.take_along_axis(i1, pos // PEER_TOPK, -1) * PEER_N_KEYS
         + jnp.take_along_axis(i2, pos % PEER_TOPK, -1))
    g = jax.nn.softmax(vals, axis=-1).astype(h.dtype)
    nc = T // PEER_CHUNK

    def chunk(args):
        xc, ec, gc = args
        pre = jnp.einsum('cd,chkd->chk', xc, u_tab[ec])
        a = jax.nn.gelu(pre, approximate=False) * gc
        return jnp.einsum('chk,chkd->cd', a, v_tab[ec])

    out = lax.map(chunk, (xt.reshape(nc, PEER_CHUNK, D),
                          e.reshape(nc, PEER_CHUNK, PEER_HEADS, PEER_TOPK),
                          g.reshape(nc, PEER_CHUNK, PEER_HEADS, PEER_TOPK)))
    return out.reshape(B, S, D)


def setup_inputs(seed: int = 0) -> dict:
    key = jax.random.key(seed)
    ks = jax.random.split(key, 17)
    f32 = jnp.float32

    def nrm(k, shape, s):
        return jax.random.normal(k, shape, f32) * s

    beta = DEEPNORM_BETA
    col_scale = jnp.concatenate([
        jnp.ones((2 * MOBA_WIDTH,), f32), jnp.full((MOBA_WIDTH,), beta, f32),
        jnp.ones((2 * DIL_WIDTH,), f32), jnp.full((DIL_WIDTH,), beta, f32)])
    return {
        "x": nrm(ks[0], (BATCH, SEQ, D_MODEL), 1.0),
        "c": nrm(ks[1], (BATCH, D_MODEL), 1.0),
        "w_ada": nrm(ks[2], (DEPTH, D_MODEL, 6 * D_MODEL), 0.2 * D_MODEL ** -0.5),
        "b_ada": nrm(ks[3], (DEPTH, 6 * D_MODEL), 0.01),
        "w_in": nrm(ks[4], (DEPTH, D_MODEL, IN_WIDTH), D_MODEL ** -0.5) * col_scale,
        "rel_bias": nrm(ks[5], (N_HEADS, REL_BUCKETS), 0.2),
        "gn_moba": 1.0 + nrm(ks[6], (DEPTH, MOBA_WIDTH), 0.02),
        "gn_dil": 1.0 + nrm(ks[7], (DEPTH, DIL_WIDTH), 0.02),
        "w_out": nrm(ks[8], (DEPTH, MIX_WIDTH, D_MODEL), beta * MIX_WIDTH ** -0.5),
        "ln1_g": 1.0 + nrm(ks[9], (DEPTH, D_MODEL), 0.02),
        "ln1_b": nrm(ks[10], (DEPTH, D_MODEL), 0.02),
        "w_q_peer": nrm(ks[11], (DEPTH, D_MODEL, PEER_HEADS * PEER_DK), D_MODEL ** -0.5),
        "sub_keys": nrm(ks[12], (DEPTH, 2, PEER_N_KEYS, PEER_DK // 2), (PEER_DK // 2) ** -0.5),
        "peer_u": nrm(ks[13], (DEPTH, PEER_N_EXPERTS, D_MODEL), beta * D_MODEL ** -0.5),
        "peer_v": nrm(ks[14], (DEPTH, PEER_N_EXPERTS, D_MODEL), beta),
        "ln2_g": 1.0 + nrm(ks[15], (DEPTH, D_MODEL), 0.02),
        "ln2_b": nrm(ks[16], (DEPTH, D_MODEL), 0.02),
    }


def reference(x, c, w_ada, b_ada, w_in, rel_bias, gn_moba, gn_dil, w_out, ln1_g, ln1_b,
              w_q_peer, sub_keys, peer_u, peer_v, ln2_g, ln2_b):
    B, S, D = x.shape
    bias_moba = rel_bias[:MOBA_HEADS]
    bias_dil = rel_bias[MOBA_HEADS:]
    c_act = jax.nn.silu(c)

    def heads(a, n):
        return a.reshape(B, S, n, HEAD_DIM).transpose(0, 2, 1, 3)

    for l in range(DEPTH):
        mod = c_act @ w_ada[l] + b_ada[l]
        sh1, sc1, g1, sh2, sc2, g2 = [m[:, None, :] for m in jnp.split(mod, 6, axis=-1)]
        h = x * (1 + sc1) + sh1
        proj = h @ w_in[l]
        qa, ka, va, qb, kb, vb = jnp.split(
            proj, np.cumsum([MOBA_WIDTH] * 3 + [DIL_WIDTH] * 2).tolist(), axis=-1)
        ya = moba_attention(heads(qa, MOBA_HEADS), heads(ka, MOBA_HEADS), heads(va, MOBA_HEADS), bias_moba)
        yb = dilated_attention(heads(qb, DIL_HEADS), heads(kb, DIL_HEADS), heads(vb, DIL_HEADS), bias_dil)
        ya = rms_norm(ya.transpose(0, 2, 1, 3).reshape(B, S, MOBA_WIDTH), gn_moba[l])
        yb = rms_norm(yb.transpose(0, 2, 1, 3).reshape(B, S, DIL_WIDTH), gn_dil[l])
        y = jnp.concatenate([ya, yb], axis=-1) @ w_out[l]
        x = layer_norm(DEEPNORM_ALPHA * x + (1 + g1) * y, ln1_g[l], ln1_b[l])
        h2 = x * (1 + sc2) + sh2
        f = peer_ffn(h2, w_q_peer[l], sub_keys[l], peer_u[l], peer_v[l])
        x = layer_norm(DEEPNORM_ALPHA * x + (1 + g2) * f, ln2_g[l], ln2_b[l])
    return x
```

```python
import functools
import math

import jax
import jax.numpy as jnp
from jax import lax
from jax.experimental import pallas as pl
from jax.experimental.pallas import tpu as pltpu

F32 = jnp.float32
BF16 = jnp.bfloat16
HIGHEST = lax.Precision.HIGHEST

HEAD_DIM = 64
MOBA_HEADS = 8
DIL_HEADS = 8
MOBA_WIDTH = MOBA_HEADS * HEAD_DIM
DIL_WIDTH = DIL_HEADS * HEAD_DIM
MOBA_BLOCK = 256
MOBA_TOPK = 3
DIL_PAIRS = ((128, 1), (512, 4), (2048, 16))
DIL_BLOCK = 128
REL_BUCKETS = 32
PEER_HEADS = 8
PEER_N_KEYS = 128
PEER_TOPK = 16
PEER_HALF = 64
DEPTH = 1
DEEPNORM_ALPHA = (2 * DEPTH) ** 0.25
LN_EPS = 1e-5
NEG_INF = -1e30
ATTN_SCALE = HEAD_DIM ** -0.5

LANES = 128
HEADS_PER_SLAB = LANES // HEAD_DIM
VMEM_LIMIT = 56 * 1024 * 1024

T5_BUCKET_LO = tuple(range(17)) + tuple(math.ceil(16 * 128 ** (k / 16) - 1e-9) for k in range(1, 16))

NT_DIMS = (((1,), (1,)), ((), ()))


def _params(*sem):
    return pltpu.CompilerParams(dimension_semantics=sem, vmem_limit_bytes=VMEM_LIMIT)


def _ada_kernel(c_ref, w_ref, b_ref, o_ref):
    c = c_ref[...]
    act = c * jax.nn.sigmoid(c)
    o_ref[...] = jnp.dot(act, w_ref[...], preferred_element_type=F32, precision=HIGHEST) + b_ref[...]


def _ada(c, w, b):
    B, D = c.shape
    n = w.shape[1] // D
    return pl.pallas_call(
        _ada_kernel,
        grid=(n,),
        in_specs=[pl.BlockSpec((B, D), lambda j: (0, 0)),
                  pl.BlockSpec((D, D), lambda j: (0, j)),
                  pl.BlockSpec((1, D), lambda j: (0, j))],
        out_specs=pl.BlockSpec((B, D), lambda j: (0, j)),
        out_shape=jax.ShapeDtypeStruct((B, n * D), F32),
        compiler_params=_params("arbitrary"),
        name="ada",
    )(c, w, b.reshape(1, -1))


def _inproj_kernel(x_ref, sh_ref, sc_ref, wqv_ref, wkd_ref, qT_ref, vT_ref, k_ref, d_ref):
    h = x_ref[0] * (1.0 + sc_ref[0, 0]) + sh_ref[0, 0]
    hb = h.astype(BF16)
    qv = lax.dot_general(wqv_ref[...], hb, NT_DIMS, preferred_element_type=F32)
    qT_ref[0] = qv[:MOBA_WIDTH].astype(BF16)
    vT_ref[0] = qv[MOBA_WIDTH:].astype(BF16)
    kd = jnp.dot(hb, wkd_ref[...], preferred_element_type=F32)
    k_ref[0] = kd[:, :MOBA_WIDTH].astype(BF16)
    d_ref[0] = kd[:, MOBA_WIDTH:]


def _inproj(x, mod4, wqvT, wkd, ts):
    B, S, D = x.shape
    nd = wkd.shape[1] - MOBA_WIDTH
    return pl.pallas_call(
        _inproj_kernel,
        grid=(B, S // ts),
        in_specs=[pl.BlockSpec((1, ts, D), lambda b, t: (b, t, 0)),
                  pl.BlockSpec((1, 1, 1, D), lambda b, t: (b, 0, 0, 0)),
                  pl.BlockSpec((1, 1, 1, D), lambda b, t: (b, 1, 0, 0)),
                  pl.BlockSpec(wqvT.shape, lambda b, t: (0, 0)),
                  pl.BlockSpec(wkd.shape, lambda b, t: (0, 0))],
        out_specs=[pl.BlockSpec((1, MOBA_WIDTH, ts), lambda b, t: (b, 0, t)),
                   pl.BlockSpec((1, MOBA_WIDTH, ts), lambda b, t: (b, 0, t)),
                   pl.BlockSpec((1, ts, MOBA_WIDTH), lambda b, t: (b, t, 0)),
                   pl.BlockSpec((1, ts, nd), lambda b, t: (b, t, 0))],
        out_shape=[jax.ShapeDtypeStruct((B, MOBA_WIDTH, S), BF16),
                   jax.ShapeDtypeStruct((B, MOBA_WIDTH, S), BF16),
                   jax.ShapeDtypeStruct((B, S, MOBA_WIDTH), BF16),
                   jax.ShapeDtypeStruct((B, S, nd), F32)],
        compiler_params=_params("arbitrary", "arbitrary"),
        name="inproj",
    )(x, mod4, mod4, wqvT, wkd)


def _t5_bucket_py(n):
    b = 0
    for i, lo in enumerate(T5_BUCKET_LO):
        if n >= lo:
            b = i
    return b


def _bias_lookup(dist, tab_ref, row, dmin, dmax):
    b_lo, b_hi = _t5_bucket_py(max(dmin, 0)), _t5_bucket_py(max(dmax, 0))
    val = jnp.full(dist.shape, tab_ref[row, b_hi], F32)
    for b in range(b_hi - 1, b_lo - 1, -1):
        val = jnp.where(dist < T5_BUCKET_LO[b + 1], tab_ref[row, b], val)
    return val


def _bias_kernel(tab_ref, moba_ref, dil_ref):
    h = pl.program_id(0)
    nb = moba_ref.shape[1]
    blk = MOBA_BLOCK
    ji = lax.broadcasted_iota(jnp.int32, (blk, blk), 0)
    ii = lax.broadcasted_iota(jnp.int32, (blk, blk), 1)
    for delta in range(nb):
        dist = delta * blk + ii - ji
        val = _bias_lookup(dist, tab_ref, h, delta * blk - (blk - 1), delta * blk + blk - 1)
        if delta == 0:
            val = jnp.where(dist >= 0, val, NEG_INF)
        moba_ref[0, delta] = val
    qi = lax.broadcasted_iota(jnp.int32, (DIL_BLOCK, 2 * DIL_BLOCK), 0)
    kj = lax.broadcasted_iota(jnp.int32, (DIL_BLOCK, 2 * DIL_BLOCK), 1)
    off = DIL_BLOCK + qi - kj
    for br, (window, dil) in enumerate(DIL_PAIRS):
        band = (off >= 0) & (off <= window // dil)
        val = _bias_lookup(off * dil, tab_ref, MOBA_HEADS + h, 0, window)
        dil_ref[0, br] = jnp.where(band, val, NEG_INF)


def _bias_tiles(rel_bias, nb):
    return pl.pallas_call(
        _bias_kernel,
        grid=(MOBA_HEADS,),
        in_specs=[pl.BlockSpec(memory_space=pltpu.SMEM)],
        out_specs=[pl.BlockSpec((1, nb, MOBA_BLOCK, MOBA_BLOCK), lambda h: (h, 0, 0, 0)),
                   pl.BlockSpec((1, len(DIL_PAIRS), DIL_BLOCK, 2 * DIL_BLOCK), lambda h: (h, 0, 0, 0))],
        out_shape=[jax.ShapeDtypeStruct((MOBA_HEADS, nb, MOBA_BLOCK, MOBA_BLOCK), F32),
                   jax.ShapeDtypeStruct((DIL_HEADS, len(DIL_PAIRS), DIL_BLOCK, 2 * DIL_BLOCK), F32)],
        compiler_params=_params("arbitrary"),
        name="bias",
    )(rel_bias)


def _moba_kernel(qT_ref, k_ref, vT_ref, bm_ref, o_ref):
    S = qT_ref.shape[2]
    blk = MOBA_BLOCK
    nb = S // blk
    qT = qT_ref[0]
    kmean = jnp.mean(k_ref[0].astype(F32).reshape(nb, blk, LANES), axis=1)
    row = lax.broadcasted_iota(jnp.int32, (LANES, S), 0)
    blk_row = lax.broadcasted_iota(jnp.int32, (nb, blk), 0)
    for e in range(HEADS_PER_SLAB):
        in_head = (row >= e * HEAD_DIM) & (row < (e + 1) * HEAD_DIM)
        qpad = jnp.where(in_head, qT, jnp.zeros_like(qT))
        gate = jnp.dot(kmean, qpad.astype(F32), preferred_element_type=F32, precision=HIGHEST)
        v_rows = slice(e * HEAD_DIM, (e + 1) * HEAD_DIM)
        for i in range(nb):
            cols = slice(i * blk, (i + 1) * blk)
            qi = qpad[:, cols]
            sel = None
            if i > MOBA_TOPK:
                g = gate[:, cols]
                rank = jnp.zeros((nb, blk), F32)
                for m in range(i):
                    gm = g[m:m + 1, :]
                    beats = (gm > g) | ((gm == g) & (blk_row > m))
                    rank = rank + beats.astype(F32)
                sel = rank < float(MOBA_TOPK)
            s = jnp.dot(k_ref[0, cols, :], qi, preferred_element_type=F32) * ATTN_SCALE + bm_ref[e, 0]
            m_run = jnp.max(s, axis=0, keepdims=True)
            p = jnp.exp(s - m_run)
            l_run = jnp.sum(p, axis=0, keepdims=True)
            acc = jnp.dot(vT_ref[0, v_rows, cols], p.astype(BF16), preferred_element_type=F32)
            for n in range(i):
                kcols = slice(n * blk, (n + 1) * blk)
                s = jnp.dot(k_ref[0, kcols, :], qi, preferred_element_type=F32) * ATTN_SCALE + bm_ref[e, i - n]
                if sel is not None:
                    s = jnp.where(sel[n:n + 1, :], s, NEG_INF)
                m_new = jnp.maximum(m_run, jnp.max(s, axis=0, keepdims=True))
                alpha = jnp.exp(m_run - m_new)
                p = jnp.exp(s - m_new)
                l_run = alpha * l_run + jnp.sum(p, axis=0, keepdims=True)
                acc = alpha * acc + jnp.dot(vT_ref[0, v_rows, kcols], p.astype(BF16), preferred_element_type=F32)
                m_run = m_new
            o_ref[0, v_rows, cols] = acc / l_run


def _moba(qT, k, vT, bm):
    B, W, S = qT.shape
    nslab = W // LANES
    nb = S // MOBA_BLOCK
    return pl.pallas_call(
        _moba_kernel,
        grid=(nslab, B),
        in_specs=[pl.BlockSpec((1, LANES, S), lambda p, b: (b, p, 0)),
                  pl.BlockSpec((1, S, LANES), lambda p, b: (b, 0, p)),
                  pl.BlockSpec((1, LANES, S), lambda p, b: (b, p, 0)),
                  pl.BlockSpec((HEADS_PER_SLAB, nb, MOBA_BLOCK, MOBA_BLOCK), lambda p, b: (p, 0, 0, 0))],
        out_specs=pl.BlockSpec((1, LANES, S), lambda p, b: (b, p, 0)),
        out_shape=jax.ShapeDtypeStruct((B, W, S), F32),
        compiler_params=_params("arbitrary", "arbitrary"),
        name="moba",
    )(qT, k, vT, bm)


def _dil_kernel(q_ref, k_ref, v_ref, bm_ref, o_ref, os_ref, ls_ref):
    S = q_ref.shape[1]
    blk = DIL_BLOCK
    lane = lax.broadcasted_iota(jnp.int32, (1, LANES), 1)
    head_lanes = [(lane >= e * HEAD_DIM) & (lane < (e + 1) * HEAD_DIM) for e in range(HEADS_PER_SLAB)]

    def rows(start, size, dil):
        return pl.ds(start, size) if dil == 1 else pl.ds(start, size, stride=dil)

    def tile(br, dil, qstart, kstart, nk):
        qs = q_ref[0, rows(qstart, blk, dil), :]
        ks = k_ref[0, rows(kstart, nk, dil), :].astype(BF16)
        vs = v_ref[0, rows(kstart, nk, dil), :].astype(BF16)
        o_tile = jnp.zeros((blk, LANES), F32)
        l_tile = jnp.zeros((blk, LANES), F32)
        for e in range(HEADS_PER_SLAB):
            qe = jnp.where(head_lanes[e], qs, 0.0).astype(BF16)
            s = lax.dot_general(qe, ks, NT_DIMS, preferred_element_type=F32) * ATTN_SCALE
            s = s + bm_ref[e, br, :, 2 * blk - nk:]
            m = jnp.max(s, axis=1, keepdims=True)
            p = jnp.exp(s - m)
            l = jnp.sum(p, axis=1, keepdims=True)
            o = jnp.dot(p.astype(BF16), vs, preferred_element_type=F32) / l
            o_tile = jnp.where(head_lanes[e], o, o_tile)
            l_tile = jnp.where(head_lanes[e], m + jnp.log(l), l_tile)
        os_ref[br, rows(qstart, blk, dil), :] = o_tile
        ls_ref[br, rows(qstart, blk, dil), :] = l_tile

    for br, (window, dil) in enumerate(DIL_PAIRS):
        sub_len = S // dil
        nblk = sub_len // blk
        span = blk * dil

        def first(r, carry, br=br, dil=dil):
            tile(br, dil, r, r, blk)
            return carry

        lax.fori_loop(0, dil, first, 0)
        if nblk > 1:
            def later(t, carry, br=br, dil=dil, nblk=nblk, span=span):
                r = t // (nblk - 1)
                n = t % (nblk - 1) + 1
                tile(br, dil, r + n * span, r + (n - 1) * span, 2 * blk)
                return carry

            lax.fori_loop(0, dil * (nblk - 1), later, 0)

    chunk = 256

    def merge(t, carry):
        rs = pl.ds(pl.multiple_of(t * chunk, chunk), chunk)
        ls = [ls_ref[br, rs, :] for br in range(len(DIL_PAIRS))]
        mx = functools.reduce(jnp.maximum, ls)
        ws = [jnp.exp(l - mx) for l in ls]
        num = functools.reduce(jnp.add, [w * os_ref[br, rs, :] for br, w in enumerate(ws)])
        o_ref[0, rs, :] = num / functools.reduce(jnp.add, ws)
        return carry

    lax.fori_loop(0, S // chunk, merge, 0)


def _dilated(qkv, bm):
    B, S, W3 = qkv.shape
    nslab = W3 // 3 // LANES
    nbr = len(DIL_PAIRS)
    return pl.pallas_call(
        _dil_kernel,
        grid=(B, nslab),
        in_specs=[pl.BlockSpec((1, S, LANES), lambda b, p: (b, 0, p)),
                  pl.BlockSpec((1, S, LANES), lambda b, p: (b, 0, nslab + p)),
                  pl.BlockSpec((1, S, LANES), lambda b, p: (b, 0, 2 * nslab + p)),
                  pl.BlockSpec((HEADS_PER_SLAB, nbr, DIL_BLOCK, 2 * DIL_BLOCK), lambda b, p: (p, 0, 0, 0))],
        out_specs=pl.BlockSpec((1, S, LANES), lambda b, p: (b, 0, p)),
        out_shape=jax.ShapeDtypeStruct((B, S, W3 // 3), F32),
        scratch_shapes=[pltpu.VMEM((nbr, S, LANES), F32), pltpu.VMEM((nbr, S, LANES), F32)],
        compiler_params=_params("arbitrary", "arbitrary"),
        name="dilated",
    )(qkv, qkv, qkv, bm)


def _rms(y, g):
    return y * lax.rsqrt(jnp.mean(y * y, axis=-1, keepdims=True) + LN_EPS) * g


def _layer_norm(z, g, b):
    mu = jnp.mean(z, axis=-1, keepdims=True)
    zc = z - mu
    var = jnp.mean(zc * zc, axis=-1, keepdims=True)
    return zc * lax.rsqrt(var + LN_EPS) * g + b


def _outproj_kernel(yaT_ref, yb_ref, x_ref, g1_ref, sh2_ref, sc2_ref, gna_ref, gnb_ref, wa_ref, wb_ref,
                    lg_ref, lb_ref, x1_ref, h2_ref):
    ya = _rms(yaT_ref[0].T, gna_ref[...])
    yb = _rms(yb_ref[0], gnb_ref[...])
    y = (jnp.dot(ya.astype(BF16), wa_ref[...], preferred_element_type=F32)
         + jnp.dot(yb.astype(BF16), wb_ref[...], preferred_element_type=F32))
    z = DEEPNORM_ALPHA * x_ref[0] + (1.0 + g1_ref[0, 0]) * y
    x1 = _layer_norm(z, lg_ref[...], lb_ref[...])
    x1_ref[0] = x1
    h2_ref[0] = (x1 * (1.0 + sc2_ref[0, 0]) + sh2_ref[0, 0]).astype(BF16)


def _outproj(yaT, yb, x, mod4, gna, gnb, wa, wb, lg, lb, ts):
    B, S, D = x.shape

    def row(a):
        return a.reshape(1, -1)

    def mod_spec(i):
        return pl.BlockSpec((1, 1, 1, D), lambda b, t, i=i: (b, i, 0, 0))

    def full(a):
        return pl.BlockSpec(a.shape, lambda b, t: (0,) * a.ndim)

    gna, gnb, lg, lb = row(gna), row(gnb), row(lg), row(lb)
    return pl.pallas_call(
        _outproj_kernel,
        grid=(B, S // ts),
        in_specs=[pl.BlockSpec((1, MOBA_WIDTH, ts), lambda b, t: (b, 0, t)),
                  pl.BlockSpec((1, ts, DIL_WIDTH), lambda b, t: (b, t, 0)),
                  pl.BlockSpec((1, ts, D), lambda b, t: (b, t, 0)),
                  mod_spec(2), mod_spec(3), mod_spec(4),
                  full(gna), full(gnb), full(wa), full(wb), full(lg), full(lb)],
        out_specs=[pl.BlockSpec((1, ts, D), lambda b, t: (b, t, 0)),
                   pl.BlockSpec((1, ts, D), lambda b, t: (b, t, 0))],
        out_shape=[jax.ShapeDtypeStruct((B, S, D), F32), jax.ShapeDtypeStruct((B, S, D), BF16)],
        compiler_params=_params("arbitrary", "arbitrary"),
        name="outproj",
    )(yaT, yb, x, mod4, mod4, mod4, gna, gnb, wa, wb, lg, lb)


_CAND_GROUP_LEN = (16, 8, 5, 4, 3, 2, 2, 2)


def _top16(s):
    n, T = s.shape
    row = lax.broadcasted_iota(jnp.int32, (n, T), 0)
    krow = lax.broadcasted_iota(jnp.int32, (PEER_TOPK, T), 0)
    rank = jnp.full((n, T), float(PEER_TOPK), F32)
    vals = jnp.zeros((PEER_TOPK, T), F32)
    for r in range(PEER_TOPK):
        m = jnp.max(s, axis=0, keepdims=True)
        idx = jnp.min(jnp.where(s == m, row, n), axis=0, keepdims=True)
        hit = row == idx
        rank = jnp.where(hit, float(r), rank)
        s = jnp.where(hit, -jnp.inf, s)
        vals = jnp.where(krow == r, m, vals)
    return vals, rank


def _peerq_kernel(h2_ref, wq_ref, k1_ref, k2_ref, p1_ref, lb_ref, p2_ref, r2_ref, q_scr):
    T = h2_ref.shape[0]
    q_scr[...] = lax.dot_general(wq_ref[...], h2_ref[...], NT_DIMS, preferred_element_type=F32)
    sub = lax.broadcasted_iota(jnp.int32, (8, T), 0)

    def head(h, carry):
        base = pl.multiple_of(h * 2 * PEER_HALF, 2 * PEER_HALF)
        s1 = jnp.dot(k1_ref[...], q_scr[pl.ds(base, PEER_HALF), :], preferred_element_type=F32, precision=HIGHEST)
        s2 = jnp.dot(k2_ref[...], q_scr[pl.ds(base + PEER_HALF, PEER_HALF), :],
                     preferred_element_type=F32, precision=HIGHEST)
        v1, r1 = _top16(s1)
        v2, r2 = _top16(s2)
        groups = [v1[0:1] + v2]
        for j in range(1, 8):
            g = v1[j:j + 1] + v2[0:8]
            groups.append(jnp.where(sub < _CAND_GROUP_LEN[j], g, -jnp.inf))
        groups.append(v1[8:16] + v2[0:1])
        cand = jnp.concatenate(groups, axis=0)
        n = cand.shape[0]
        row = lax.broadcasted_iota(jnp.int32, (n, T), 0)
        work = cand
        picked = jnp.zeros((n, T), F32)
        for _ in range(PEER_TOPK):
            m = jnp.max(work, axis=0, keepdims=True)
            idx = jnp.min(jnp.where(work == m, row, n), axis=0, keepdims=True)
            hit = row == idx
            picked = jnp.where(hit, 1.0, picked)
            work = jnp.where(hit, -jnp.inf, work)
        lb = jnp.zeros_like(r1)
        bounds = [0, 16, 24, 32, 40, 48, 56, 64, 72]
        for j in range(8):
            cnt = jnp.sum(picked[bounds[j]:bounds[j + 1]], axis=0, keepdims=True)
            lb = jnp.where(r1 == float(j), cnt, lb)
        for j in range(8, PEER_TOPK):
            lb = jnp.where(r1 == float(j), picked[64 + j:65 + j], lb)
        z = jnp.sum(jnp.where(picked > 0.5, jnp.exp(cand - cand[0:1]), 0.0), axis=0, keepdims=True)
        p1_ref[h] = jnp.exp(s1 - v1[0:1]) / z
        lb_ref[h] = lb
        p2_ref[h] = jnp.exp(s2 - v2[0:1])
        r2_ref[h] = r2
        return carry

    lax.fori_loop(0, PEER_HEADS, head, 0)


def _peerq(h2, wqT, k1, k2, tm):
    T, D = h2.shape
    gshape = jax.ShapeDtypeStruct((PEER_HEADS, PEER_N_KEYS, T), F32)
    gspec = pl.BlockSpec((PEER_HEADS, PEER_N_KEYS, tm), lambda t: (0, 0, t))
    return pl.pallas_call(
        _peerq_kernel,
        grid=(T // tm,),
        in_specs=[pl.BlockSpec((tm, D), lambda t: (t, 0)),
                  pl.BlockSpec(wqT.shape, lambda t: (0, 0)),
                  pl.BlockSpec(k1.shape, lambda t: (0, 0)),
                  pl.BlockSpec(k2.shape, lambda t: (0, 0))],
        out_specs=[gspec] * 4,
        out_shape=[gshape] * 4,
        scratch_shapes=[pltpu.VMEM((wqT.shape[0], tm), F32)],
        compiler_params=_params("arbitrary"),
        name="peerq",
    )(h2, wqT, k1, k2)


def _gelu(x):
    return 0.5 * x * (1.0 + lax.erf(x * math.sqrt(0.5)))


def _peerx_kernel(h2_ref, u_ref, vT_ref, p1_ref, lb_ref, p2_ref, r2_ref, x1_ref, g2_ref, lg_ref, lbias_ref,
                  o_ref, acc_ref, a_ref):
    c = pl.program_id(1)
    nblk = u_ref.shape[0] // PEER_N_KEYS

    @pl.when(c == 0)
    def _():
        acc_ref[...] = jnp.zeros_like(acc_ref)

    pre = lax.dot_general(u_ref[...], h2_ref[...], NT_DIMS, preferred_element_type=F32)
    for ii in range(nblk):
        i1 = c * nblk + ii
        gate = None
        for h in range(PEER_HEADS):
            w = jnp.where(r2_ref[h] < lb_ref[h, pl.ds(i1, 1), :], p1_ref[h, pl.ds(i1, 1), :] * p2_ref[h], 0.0)
            gate = w if gate is None else gate + w
        rows = slice(ii * PEER_N_KEYS, (ii + 1) * PEER_N_KEYS)
        a_ref[rows, :] = (_gelu(pre[rows, :]) * gate).astype(BF16)
    acc_ref[...] += jnp.dot(vT_ref[...], a_ref[...], preferred_element_type=F32)

    @pl.when(c == pl.num_programs(1) - 1)
    def _():
        f = acc_ref[...].T
        z = DEEPNORM_ALPHA * x1_ref[...] + (1.0 + g2_ref[0, 0]) * f
        o_ref[...] = _layer_norm(z, lg_ref[...], lbias_ref[...])


def _peerx(h2, u, vT, gates, x1, mod4, lg, lb, seq, tm, te):
    T, D = h2.shape
    E = u.shape[0]
    gspec = pl.BlockSpec((PEER_HEADS, PEER_N_KEYS, tm), lambda t, c: (0, 0, t))
    lg, lb = lg.reshape(1, -1), lb.reshape(1, -1)
    return pl.pallas_call(
        _peerx_kernel,
        grid=(T // tm, E // te),
        in_specs=[pl.BlockSpec((tm, D), lambda t, c: (t, 0)),
                  pl.BlockSpec((te, D), lambda t, c: (c, 0)),
                  pl.BlockSpec((D, te), lambda t, c: (0, c)),
                  gspec, gspec, gspec, gspec,
                  pl.BlockSpec((tm, D), lambda t, c: (t, 0)),
                  pl.BlockSpec((1, 1, 1, D), lambda t, c: ((t * tm) // seq, 5, 0, 0)),
                  pl.BlockSpec(lg.shape, lambda t, c: (0, 0)),
                  pl.BlockSpec(lb.shape, lambda t, c: (0, 0))],
        out_specs=pl.BlockSpec((tm, D), lambda t, c: (t, 0)),
        out_shape=jax.ShapeDtypeStruct((T, D), F32),
        scratch_shapes=[pltpu.VMEM((D, tm), F32), pltpu.VMEM((te, tm), BF16)],
        compiler_params=_params("arbitrary", "arbitrary"),
        name="peerx",
    )(h2, u, vT, *gates, x1, mod4, lg, lb)


def kernel(x, c, w_ada, b_ada, w_in, rel_bias, gn_moba, gn_dil, w_out, ln1_g, ln1_b, w_q_peer, sub_keys,
           peer_u, peer_v, ln2_g, ln2_b):
    B, S, D = x.shape
    assert w_ada.shape[0] == DEPTH == 1
    assert S % (DIL_BLOCK * max(d for _, d in DIL_PAIRS)) == 0 and S % MOBA_BLOCK == 0
    nb = S // MOBA_BLOCK

    mod = _ada(c, w_ada[0], b_ada[0])
    mod4 = mod.reshape(B, 6, 1, D)

    w = w_in[0]
    qa, ka, va, dil = (w[:, :MOBA_WIDTH], w[:, MOBA_WIDTH:2 * MOBA_WIDTH], w[:, 2 * MOBA_WIDTH:3 * MOBA_WIDTH],
                       w[:, 3 * MOBA_WIDTH:])
    wqvT = jnp.concatenate([qa, va], axis=1).T.astype(BF16)
    wkd = jnp.concatenate([ka, dil], axis=1).astype(BF16)
    wo = w_out[0].astype(BF16)
    wqT = w_q_peer[0].T.astype(BF16)
    u = peer_u[0].astype(BF16)
    vT = peer_v[0].T.astype(BF16)

    qT, vTm, km, qkv_dil = _inproj(x, mod4, wqvT, wkd, ts=512)
    bm_moba, bm_dil = _bias_tiles(rel_bias, nb)
    yaT = _moba(qT, km, vTm, bm_moba)
    yb = _dilated(qkv_dil, bm_dil)
    x1, h2 = _outproj(yaT, yb, x, mod4, gn_moba[0], gn_dil[0], wo[:MOBA_WIDTH], wo[MOBA_WIDTH:],
                      ln1_g[0], ln1_b[0], ts=512)
    T = B * S
    x1 = x1.reshape(T, D)
    h2 = h2.reshape(T, D)
    gates = _peerq(h2, wqT, sub_keys[0, 0], sub_keys[0, 1], tm=256)
    out = _peerx(h2, u, vT, gates, x1, mod4, ln2_g[0], ln2_b[0], seq=S, tm=512, te=512)
    return out.reshape(B, S, D)
```

```python
import functools
import math

import jax
import jax.numpy as jnp
from jax import lax
from jax.experimental import pallas as pl
from jax.experimental.pallas import tpu as pltpu

F32 = jnp.float32
BF16 = jnp.bfloat16
HIGHEST = lax.Precision.HIGHEST

HEAD_DIM = 64
MOBA_HEADS = 8
DIL_HEADS = 8
MOBA_WIDTH = MOBA_HEADS * HEAD_DIM
DIL_WIDTH = DIL_HEADS * HEAD_DIM
MOBA_BLOCK = 256
MOBA_TOPK = 3
DIL_PAIRS = ((128, 1), (512, 4), (2048, 16))
DIL_BLOCK = 128
DIL_TILE_UNROLL = 4
REL_BUCKETS = 32
PEER_HEADS = 8
PEER_N_KEYS = 128
PEER_TOPK = 16
PEER_HALF = 64
DEPTH = 1
DEEPNORM_ALPHA = (2 * DEPTH) ** 0.25
LN_EPS = 1e-5
NEG_INF = -1e30
ATTN_SCALE = HEAD_DIM ** -0.5

LANES = 128
HEADS_PER_SLAB = LANES // HEAD_DIM
VMEM_LIMIT = 56 * 1024 * 1024

T5_BUCKET_LO = tuple(range(17)) + tuple(math.ceil(16 * 128 ** (k / 16) - 1e-9) for k in range(1, 16))

NT_DIMS = (((1,), (1,)), ((), ()))


def _params(*sem):
    return pltpu.CompilerParams(dimension_semantics=sem, vmem_limit_bytes=VMEM_LIMIT)


def _ada_kernel(c_ref, w_ref, b_ref, o_ref):
    c = c_ref[...]
    act = c * jax.nn.sigmoid(c)
    o_ref[...] = jnp.dot(act, w_ref[...], preferred_element_type=F32, precision=HIGHEST) + b_ref[...]


def _ada(c, w, b):
    B, D = c.shape
    n = w.shape[1] // D
    return pl.pallas_call(
        _ada_kernel,
        grid=(n,),
        in_specs=[pl.BlockSpec((B, D), lambda j: (0, 0)),
                  pl.BlockSpec((D, D), lambda j: (0, j)),
                  pl.BlockSpec((1, D), lambda j: (0, j))],
        out_specs=pl.BlockSpec((B, D), lambda j: (0, j)),
        out_shape=jax.ShapeDtypeStruct((B, n * D), F32),
        compiler_params=_params("arbitrary"),
        name="ada",
    )(c, w, b.reshape(1, -1))


def _inproj_kernel(x_ref, sh_ref, sc_ref, wqv_ref, wkd_ref, qT_ref, vT_ref, k_ref, d_ref):
    h = x_ref[0] * (1.0 + sc_ref[0, 0]) + sh_ref[0, 0]
    hb = h.astype(BF16)
    qv = lax.dot_general(wqv_ref[...], hb, NT_DIMS, preferred_element_type=F32)
    qT_ref[0] = qv[:MOBA_WIDTH].astype(BF16)
    vT_ref[0] = qv[MOBA_WIDTH:].astype(BF16)
    kd = jnp.dot(hb, wkd_ref[...], preferred_element_type=F32)
    k_ref[0] = kd[:, :MOBA_WIDTH].astype(BF16)
    d_ref[0] = kd[:, MOBA_WIDTH:]


def _inproj(x, mod4, wqvT, wkd, ts):
    B, S, D = x.shape
    nd = wkd.shape[1] - MOBA_WIDTH
    return pl.pallas_call(
        _inproj_kernel,
        grid=(B, S // ts),
        in_specs=[pl.BlockSpec((1, ts, D), lambda b, t: (b, t, 0)),
                  pl.BlockSpec((1, 1, 1, D), lambda b, t: (b, 0, 0, 0)),
                  pl.BlockSpec((1, 1, 1, D), lambda b, t: (b, 1, 0, 0)),
                  pl.BlockSpec(wqvT.shape, lambda b, t: (0, 0)),
                  pl.BlockSpec(wkd.shape, lambda b, t: (0, 0))],
        out_specs=[pl.BlockSpec((1, MOBA_WIDTH, ts), lambda b, t: (b, 0, t)),
                   pl.BlockSpec((1, MOBA_WIDTH, ts), lambda b, t: (b, 0, t)),
                   pl.BlockSpec((1, ts, MOBA_WIDTH), lambda b, t: (b, t, 0)),
                   pl.BlockSpec((1, ts, nd), lambda b, t: (b, t, 0))],
        out_shape=[jax.ShapeDtypeStruct((B, MOBA_WIDTH, S), BF16),
                   jax.ShapeDtypeStruct((B, MOBA_WIDTH, S), BF16),
                   jax.ShapeDtypeStruct((B, S, MOBA_WIDTH), BF16),
                   jax.ShapeDtypeStruct((B, S, nd), F32)],
        compiler_params=_params("arbitrary", "arbitrary"),
        name="inproj",
    )(x, mod4, mod4, wqvT, wkd)


def _t5_bucket_py(n):
    b = 0
    for i, lo in enumerate(T5_BUCKET_LO):
        if n >= lo:
            b = i
    return b


def _bias_lookup(dist, tab_ref, row, dmin, dmax):
    b_lo, b_hi = _t5_bucket_py(max(dmin, 0)), _t5_bucket_py(max(dmax, 0))
    val = jnp.full(dist.shape, tab_ref[row, b_hi], F32)
    for b in range(b_hi - 1, b_lo - 1, -1):
        val = jnp.where(dist < T5_BUCKET_LO[b + 1], tab_ref[row, b], val)
    return val


def _bias_kernel(tab_ref, moba_ref, dil_ref):
    h = pl.program_id(0)
    nb = moba_ref.shape[1]
    blk = MOBA_BLOCK
    ji = lax.broadcasted_iota(jnp.int32, (blk, blk), 0)
    ii = lax.broadcasted_iota(jnp.int32, (blk, blk), 1)
    for delta in range(nb):
        dist = delta * blk + ii - ji
        val = _bias_lookup(dist, tab_ref, h, delta * blk - (blk - 1), delta * blk + blk - 1)
        if delta == 0:
            val = jnp.where(dist >= 0, val, NEG_INF)
        moba_ref[0, delta] = val
    qi = lax.broadcasted_iota(jnp.int32, (DIL_BLOCK, 2 * DIL_BLOCK), 0)
    kj = lax.broadcasted_iota(jnp.int32, (DIL_BLOCK, 2 * DIL_BLOCK), 1)
    off = DIL_BLOCK + qi - kj
    for br, (window, dil) in enumerate(DIL_PAIRS):
        band = (off >= 0) & (off <= window // dil)
        val = _bias_lookup(off * dil, tab_ref, MOBA_HEADS + h, 0, window)
        dil_ref[0, br] = jnp.where(band, val, NEG_INF)


def _bias_tiles(rel_bias, nb):
    return pl.pallas_call(
        _bias_kernel,
        grid=(MOBA_HEADS,),
        in_specs=[pl.BlockSpec(memory_space=pltpu.SMEM)],
        out_specs=[pl.BlockSpec((1, nb, MOBA_BLOCK, MOBA_BLOCK), lambda h: (h, 0, 0, 0)),
                   pl.BlockSpec((1, len(DIL_PAIRS), DIL_BLOCK, 2 * DIL_BLOCK), lambda h: (h, 0, 0, 0))],
        out_shape=[jax.ShapeDtypeStruct((MOBA_HEADS, nb, MOBA_BLOCK, MOBA_BLOCK), F32),
                   jax.ShapeDtypeStruct((DIL_HEADS, len(DIL_PAIRS), DIL_BLOCK, 2 * DIL_BLOCK), F32)],
        compiler_params=_params("arbitrary"),
        name="bias",
    )(rel_bias)


def _moba_kernel(qT_ref, k_ref, vT_ref, bm_ref, o_ref):
    S = qT_ref.shape[2]
    blk = MOBA_BLOCK
    nb = S // blk
    qT = qT_ref[0]
    kmean = jnp.mean(k_ref[0].astype(F32).reshape(nb, blk, LANES), axis=1)
    row = lax.broadcasted_iota(jnp.int32, (LANES, S), 0)
    blk_row = lax.broadcasted_iota(jnp.int32, (nb, blk), 0)
    for e in range(HEADS_PER_SLAB):
        in_head = (row >= e * HEAD_DIM) & (row < (e + 1) * HEAD_DIM)
        qpad = jnp.where(in_head, qT, jnp.zeros_like(qT))
        gate = jnp.dot(kmean, qpad.astype(F32), preferred_element_type=F32, precision=HIGHEST)
        v_rows = slice(e * HEAD_DIM, (e + 1) * HEAD_DIM)
        for i in range(nb):
            cols = slice(i * blk, (i + 1) * blk)
            qi = qpad[:, cols]
            penalty = None
            if i > MOBA_TOPK:
                g = gate[:, cols]
                rank = jnp.zeros((nb, blk), F32)
                for m in range(i):
                    gm = g[m:m + 1, :]
                    beats = (gm > g) | ((gm == g) & (blk_row > m))
                    rank = rank + beats.astype(F32)
                penalty = jnp.where(rank < float(MOBA_TOPK), 0.0, NEG_INF)
            s = jnp.dot(k_ref[0, cols, :], qi, preferred_element_type=F32) * ATTN_SCALE + bm_ref[e, 0]
            m_run = jnp.max(s, axis=0, keepdims=True)
            p = jnp.exp(s - m_run)
            l_run = jnp.sum(p, axis=0, keepdims=True)
            acc = jnp.dot(vT_ref[0, v_rows, cols], p.astype(BF16), preferred_element_type=F32)
            for n in range(i):
                kcols = slice(n * blk, (n + 1) * blk)
                s = jnp.dot(k_ref[0, kcols, :], qi, preferred_element_type=F32) * ATTN_SCALE + bm_ref[e, i - n]
                if penalty is not None:
                    s = s + penalty[n:n + 1, :]
                m_new = jnp.maximum(m_run, jnp.max(s, axis=0, keepdims=True))
                alpha = jnp.exp(m_run - m_new)
                p = jnp.exp(s - m_new)
                l_run = alpha * l_run + jnp.sum(p, axis=0, keepdims=True)
                acc = alpha * acc + jnp.dot(vT_ref[0, v_rows, kcols], p.astype(BF16), preferred_element_type=F32)
                m_run = m_new
            o_ref[0, v_rows, cols] = acc / l_run


def _moba(qT, k, vT, bm):
    B, W, S = qT.shape
    nslab = W // LANES
    nb = S // MOBA_BLOCK
    return pl.pallas_call(
        _moba_kernel,
        grid=(nslab, B),
        in_specs=[pl.BlockSpec((1, LANES, S), lambda p, b: (b, p, 0)),
                  pl.BlockSpec((1, S, LANES), lambda p, b: (b, 0, p)),
                  pl.BlockSpec((1, LANES, S), lambda p, b: (b, p, 0)),
                  pl.BlockSpec((HEADS_PER_SLAB, nb, MOBA_BLOCK, MOBA_BLOCK), lambda p, b: (p, 0, 0, 0))],
        out_specs=pl.BlockSpec((1, LANES, S), lambda p, b: (b, p, 0)),
        out_shape=jax.ShapeDtypeStruct((B, W, S), F32),
        compiler_params=_params("arbitrary", "arbitrary"),
        name="moba",
    )(qT, k, vT, bm)


def _dil_kernel(q_ref, k_ref, v_ref, bm_ref, o_ref, os_ref, ls_ref):
    S = q_ref.shape[1]
    blk = DIL_BLOCK
    lane = lax.broadcasted_iota(jnp.int32, (1, LANES), 1)
    head_lanes = [(lane >= e * HEAD_DIM) & (lane < (e + 1) * HEAD_DIM) for e in range(HEADS_PER_SLAB)]

    def rows(start, size, dil):
        return pl.ds(start, size) if dil == 1 else pl.ds(start, size, stride=dil)

    def tile(br, dil, qstart, kstart, nk):
        qs = q_ref[0, rows(qstart, blk, dil), :]
        ks = k_ref[0, rows(kstart, nk, dil), :].astype(BF16)
        vs = v_ref[0, rows(kstart, nk, dil), :].astype(BF16)
        o_tile = jnp.zeros((blk, LANES), F32)
        l_tile = jnp.zeros((blk, LANES), F32)
        for e in range(HEADS_PER_SLAB):
            qe = jnp.where(head_lanes[e], qs, 0.0).astype(BF16)
            s = lax.dot_general(qe, ks, NT_DIMS, preferred_element_type=F32) * ATTN_SCALE
            s = s + bm_ref[e, br, :, 2 * blk - nk:]
            m = jnp.max(s, axis=1, keepdims=True)
            p = jnp.exp(s - m)
            l = jnp.sum(p, axis=1, keepdims=True)
            o = jnp.dot(p.astype(BF16), vs, preferred_element_type=F32) / l
            o_tile = jnp.where(head_lanes[e], o, o_tile)
            l_tile = jnp.where(head_lanes[e], m + jnp.log(l), l_tile)
        os_ref[br, rows(qstart, blk, dil), :] = o_tile
        ls_ref[br, rows(qstart, blk, dil), :] = l_tile

    for br, (window, dil) in enumerate(DIL_PAIRS):
        sub_len = S // dil
        nblk = sub_len // blk
        span = blk * dil

        def first(r, carry, br=br, dil=dil):
            tile(br, dil, r, r, blk)
            return carry

        lax.fori_loop(0, dil, first, 0, unroll=min(dil, DIL_TILE_UNROLL))
        if nblk > 1:
            def later(t, carry, br=br, dil=dil, nblk=nblk, span=span):
                r = t // (nblk - 1)
                n = t % (nblk - 1) + 1
                tile(br, dil, r + n * span, r + (n - 1) * span, 2 * blk)
                return carry

            lax.fori_loop(0, dil * (nblk - 1), later, 0, unroll=DIL_TILE_UNROLL - 1)

    chunk = 256

    def merge(t, carry):
        rs = pl.ds(pl.multiple_of(t * chunk, chunk), chunk)
        ls = [ls_ref[br, rs, :] for br in range(len(DIL_PAIRS))]
        mx = functools.reduce(jnp.maximum, ls)
        ws = [jnp.exp(l - mx) for l in ls]
        num = functools.reduce(jnp.add, [w * os_ref[br, rs, :] for br, w in enumerate(ws)])
        o_ref[0, rs, :] = num / functools.reduce(jnp.add, ws)
        return carry

    lax.fori_loop(0, S // chunk, merge, 0)


def _dilated(qkv, bm):
    B, S, W3 = qkv.shape
    nslab = W3 // 3 // LANES
    nbr = len(DIL_PAIRS)
    return pl.pallas_call(
        _dil_kernel,
        grid=(B, nslab),
        in_specs=[pl.BlockSpec((1, S, LANES), lambda b, p: (b, 0, p)),
                  pl.BlockSpec((1, S, LANES), lambda b, p: (b, 0, nslab + p)),
                  pl.BlockSpec((1, S, LANES), lambda b, p: (b, 0, 2 * nslab + p)),
                  pl.BlockSpec((HEADS_PER_SLAB, nbr, DIL_BLOCK, 2 * DIL_BLOCK), lambda b, p: (p, 0, 0, 0))],
        out_specs=pl.BlockSpec((1, S, LANES), lambda b, p: (b, 0, p)),
        out_shape=jax.ShapeDtypeStruct((B, S, W3 // 3), F32),
        scratch_shapes=[pltpu.VMEM((nbr, S, LANES), F32), pltpu.VMEM((nbr, S, LANES), F32)],
        compiler_params=_params("arbitrary", "arbitrary"),
        name="dilated",
    )(qkv, qkv, qkv, bm)


def _rms(y, g):
    return y * lax.rsqrt(jnp.mean(y * y, axis=-1, keepdims=True) + LN_EPS) * g


def _layer_norm(z, g, b):
    mu = jnp.mean(z, axis=-1, keepdims=True)
    zc = z - mu
    var = jnp.mean(zc * zc, axis=-1, keepdims=True)
    return zc * lax.rsqrt(var + LN_EPS) * g + b


def _outproj_kernel(yaT_ref, yb_ref, x_ref, g1_ref, sh2_ref, sc2_ref, gna_ref, gnb_ref, wa_ref, wb_ref,
                    lg_ref, lb_ref, x1_ref, h2_ref):
    ya = _rms(yaT_ref[0].T, gna_ref[...])
    yb = _rms(yb_ref[0], gnb_ref[...])
    y = (jnp.dot(ya.astype(BF16), wa_ref[...], preferred_element_type=F32)
         + jnp.dot(yb.astype(BF16), wb_ref[...], preferred_element_type=F32))
    z = DEEPNORM_ALPHA * x_ref[0] + (1.0 + g1_ref[0, 0]) * y
    x1 = _layer_norm(z, lg_ref[...], lb_ref[...])
    x1_ref[0] = x1
    h2_ref[0] = (x1 * (1.0 + sc2_ref[0, 0]) + sh2_ref[0, 0]).astype(BF16)


def _outproj(yaT, yb, x, mod4, gna, gnb, wa, wb, lg, lb, ts):
    B, S, D = x.shape

    def row(a):
        return a.reshape(1, -1)

    def mod_spec(i):
        return pl.BlockSpec((1, 1, 1, D), lambda b, t, i=i: (b, i, 0, 0))

    def full(a):
        return pl.BlockSpec(a.shape, lambda b, t: (0,) * a.ndim)

    gna, gnb, lg, lb = row(gna), row(gnb), row(lg), row(lb)
    return pl.pallas_call(
        _outproj_kernel,
        grid=(B, S // ts),
        in_specs=[pl.BlockSpec((1, MOBA_WIDTH, ts), lambda b, t: (b, 0, t)),
                  pl.BlockSpec((1, ts, DIL_WIDTH), lambda b, t: (b, t, 0)),
                  pl.BlockSpec((1, ts, D), lambda b, t: (b, t, 0)),
                  mod_spec(2), mod_spec(3), mod_spec(4),
                  full(gna), full(gnb), full(wa), full(wb), full(lg), full(lb)],
        out_specs=[pl.BlockSpec((1, ts, D), lambda b, t: (b, t, 0)),
                   pl.BlockSpec((1, ts, D), lambda b, t: (b, t, 0))],
        out_shape=[jax.ShapeDtypeStruct((B, S, D), F32), jax.ShapeDtypeStruct((B, S, D), BF16)],
        compiler_params=_params("arbitrary", "arbitrary"),
        name="outproj",
    )(yaT, yb, x, mod4, mod4, mod4, gna, gnb, wa, wb, lg, lb)


_CAND_GROUP_LEN = (16, 8, 5, 4, 3, 2, 2, 2)


_REMOVED = -(2.0 ** 126)


def _removed_code(r):
    return _REMOVED * (1.0 + r / 32.0)


def _top16(s):
    n, T = s.shape
    row = lax.broadcasted_iota(jnp.int32, (n, T), 0).astype(F32)
    krow = lax.broadcasted_iota(jnp.int32, (PEER_TOPK, T), 0)
    vals = jnp.zeros((PEER_TOPK, T), F32)
    for r in range(PEER_TOPK):
        m = jnp.max(s, axis=0, keepdims=True)
        idx = jnp.min(jnp.where(s == m, row, float(n)), axis=0, keepdims=True)
        s = jnp.where(row == idx, _removed_code(r), s)
        vals = jnp.where(krow == r, m, vals)
    rank = jnp.where(s <= _REMOVED, (s * (1.0 / _REMOVED) - 1.0) * 32.0, float(PEER_TOPK))
    return vals, rank


def _peerq_kernel(h2_ref, wq_ref, k1_ref, k2_ref, p1_ref, lb_ref, p2_ref, r2_ref, q_scr):
    T = h2_ref.shape[0]
    q_scr[...] = lax.dot_general(wq_ref[...], h2_ref[...], NT_DIMS, preferred_element_type=F32)
    sub = lax.broadcasted_iota(jnp.int32, (8, T), 0)

    def head(h, carry):
        base = pl.multiple_of(h * 2 * PEER_HALF, 2 * PEER_HALF)
        s1 = jnp.dot(k1_ref[...], q_scr[pl.ds(base, PEER_HALF), :], preferred_element_type=F32, precision=HIGHEST)
        s2 = jnp.dot(k2_ref[...], q_scr[pl.ds(base + PEER_HALF, PEER_HALF), :],
                     preferred_element_type=F32, precision=HIGHEST)
        v1, r1 = _top16(s1)
        v2, r2 = _top16(s2)
        groups = [v1[0:1] + v2]
        for j in range(1, 8):
            g = v1[j:j + 1] + v2[0:8]
            groups.append(jnp.where(sub < _CAND_GROUP_LEN[j], g, -jnp.inf))
        groups.append(v1[8:16] + v2[0:1])
        cand = jnp.concatenate(groups, axis=0)
        n = cand.shape[0]
        row = lax.broadcasted_iota(jnp.int32, (n, T), 0).astype(F32)
        work = cand
        for r in range(PEER_TOPK):
            m = jnp.max(work, axis=0, keepdims=True)
            idx = jnp.min(jnp.where(work == m, row, float(n)), axis=0, keepdims=True)
            work = jnp.where(row == idx, _removed_code(r), work)
        picked = jnp.where((work <= _REMOVED) & (work >= _removed_code(PEER_TOPK)), 1.0, 0.0)
        lb = jnp.zeros_like(r1)
        bounds = [0, 16, 24, 32, 40, 48, 56, 64, 72]
        for j in range(8):
            cnt = jnp.sum(picked[bounds[j]:bounds[j + 1]], axis=0, keepdims=True)
            lb = jnp.where(r1 == float(j), cnt, lb)
        for j in range(8, PEER_TOPK):
            lb = jnp.where(r1 == float(j), picked[64 + j:65 + j], lb)
        z = jnp.sum(jnp.where(picked > 0.5, jnp.exp(cand - cand[0:1]), 0.0), axis=0, keepdims=True)
        p1_ref[h] = jnp.exp(s1 - v1[0:1]) / z
        lb_ref[h] = lb
        p2_ref[h] = jnp.exp(s2 - v2[0:1])
        r2_ref[h] = r2
        return carry

    lax.fori_loop(0, PEER_HEADS, head, 0)


def _peerq(h2, wqT, k1, k2, tm):
    T, D = h2.shape
    gshape = jax.ShapeDtypeStruct((PEER_HEADS, PEER_N_KEYS, T), F32)
    gspec = pl.BlockSpec((PEER_HEADS, PEER_N_KEYS, tm), lambda t: (0, 0, t))
    return pl.pallas_call(
        _peerq_kernel,
        grid=(T // tm,),
        in_specs=[pl.BlockSpec((tm, D), lambda t: (t, 0)),
                  pl.BlockSpec(wqT.shape, lambda t: (0, 0)),
                  pl.BlockSpec(k1.shape, lambda t: (0, 0)),
                  pl.BlockSpec(k2.shape, lambda t: (0, 0))],
        out_specs=[gspec] * 4,
        out_shape=[gshape] * 4,
        scratch_shapes=[pltpu.VMEM((wqT.shape[0], tm), F32)],
        compiler_params=_params("arbitrary"),
        name="peerq",
    )(h2, wqT, k1, k2)


def _gelu(x):
    return 0.5 * x * (1.0 + lax.erf(x * math.sqrt(0.5)))


def _peerx_kernel(h2_ref, u_ref, vT_ref, p1_ref, lb_ref, p2_ref, r2_ref, x1_ref, g2_ref, lg_ref, lbias_ref,
                  o_ref, acc_ref, a_ref, pre_ref):
    c = pl.program_id(1)
    nblk = u_ref.shape[0] // PEER_N_KEYS

    @pl.when(c == 0)
    def _():
        acc_ref[...] = jnp.zeros_like(acc_ref)

    pre_ref[...] = lax.dot_general(u_ref[...], h2_ref[...], NT_DIMS, preferred_element_type=F32)
    for ii in range(nblk):
        i1 = c * nblk + ii
        rows = slice(ii * PEER_N_KEYS, (ii + 1) * PEER_N_KEYS)
        lb_rows = [lb_ref[h, pl.ds(i1, 1), :] for h in range(PEER_HEADS)]
        p1_rows = [p1_ref[h, pl.ds(i1, 1), :] for h in range(PEER_HEADS)]
        for lt in range(h2_ref.shape[0] // LANES):
            cols = slice(lt * LANES, (lt + 1) * LANES)
            gate = None
            for h in range(PEER_HEADS):
                w = jnp.where(r2_ref[h, :, cols] < lb_rows[h][:, cols], p1_rows[h][:, cols] * p2_ref[h, :, cols], 0.0)
                gate = w if gate is None else gate + w
            a_ref[rows, cols] = (_gelu(pre_ref[rows, cols]) * gate).astype(BF16)
    acc_ref[...] += jnp.dot(vT_ref[...], a_ref[...], preferred_element_type=F32)

    @pl.when(c == pl.num_programs(1) - 1)
    def _():
        f = acc_ref[...].T
        z = DEEPNORM_ALPHA * x1_ref[...] + (1.0 + g2_ref[0, 0]) * f
        o_ref[...] = _layer_norm(z, lg_ref[...], lbias_ref[...])


def _peerx(h2, u, vT, gates, x1, mod4, lg, lb, seq, tm, te):
    T, D = h2.shape
    E = u.shape[0]
    gspec = pl.BlockSpec((PEER_HEADS, PEER_N_KEYS, tm), lambda t, c: (0, 0, t))
    lg, lb = lg.reshape(1, -1), lb.reshape(1, -1)
    return pl.pallas_call(
        _peerx_kernel,
        grid=(T // tm, E // te),
        in_specs=[pl.BlockSpec((tm, D), lambda t, c: (t, 0)),
                  pl.BlockSpec((te, D), lambda t, c: (c, 0)),
                  pl.BlockSpec((D, te), lambda t, c: (0, c)),
                  gspec, gspec, gspec, gspec,
                  pl.BlockSpec((tm, D), lambda t, c: (t, 0)),
                  pl.BlockSpec((1, 1, 1, D), lambda t, c: ((t * tm) // seq, 5, 0, 0)),
                  pl.BlockSpec(lg.shape, lambda t, c: (0, 0)),
                  pl.BlockSpec(lb.shape, lambda t, c: (0, 0))],
        out_specs=pl.BlockSpec((tm, D), lambda t, c: (t, 0)),
        out_shape=jax.ShapeDtypeStruct((T, D), F32),
        scratch_shapes=[pltpu.VMEM((D, tm), F32), pltpu.VMEM((te, tm), BF16), pltpu.VMEM((te, tm), F32)],
        compiler_params=_params("arbitrary", "arbitrary"),
        name="peerx",
    )(h2, u, vT, *gates, x1, mod4, lg, lb)


def kernel(x, c, w_ada, b_ada, w_in, rel_bias, gn_moba, gn_dil, w_out, ln1_g, ln1_b, w_q_peer, sub_keys,
           peer_u, peer_v, ln2_g, ln2_b):
    B, S, D = x.shape
    assert w_ada.shape[0] == DEPTH == 1
    assert S % (DIL_BLOCK * max(d for _, d in DIL_PAIRS)) == 0 and S % MOBA_BLOCK == 0
    nb = S // MOBA_BLOCK

    mod = _ada(c, w_ada[0], b_ada[0])
    mod4 = mod.reshape(B, 6, 1, D)

    w = w_in[0]
    qa, ka, va, dil = (w[:, :MOBA_WIDTH], w[:, MOBA_WIDTH:2 * MOBA_WIDTH], w[:, 2 * MOBA_WIDTH:3 * MOBA_WIDTH],
                       w[:, 3 * MOBA_WIDTH:])
    wqvT = jnp.concatenate([qa, va], axis=1).T.astype(BF16)
    wkd = jnp.concatenate([ka, dil], axis=1).astype(BF16)
    wo = w_out[0].astype(BF16)
    wqT = w_q_peer[0].T.astype(BF16)
    u = peer_u[0].astype(BF16)
    vT = peer_v[0].T.astype(BF16)

    qT, vTm, km, qkv_dil = _inproj(x, mod4, wqvT, wkd, ts=512)
    bm_moba, bm_dil = _bias_tiles(rel_bias, nb)
    yaT = _moba(qT, km, vTm, bm_moba)
    yb = _dilated(qkv_dil, bm_dil)
    x1, h2 = _outproj(yaT, yb, x, mod4, gn_moba[0], gn_dil[0], wo[:MOBA_WIDTH], wo[MOBA_WIDTH:],
                      ln1_g[0], ln1_b[0], ts=512)
    T = B * S
    x1 = x1.reshape(T, D)
    h2 = h2.reshape(T, D)
    gates = _peerq(h2, wqT, sub_keys[0, 0], sub_keys[0, 1], tm=256)
    out = _peerx(h2, u, vT, gates, x1, mod4, ln2_g[0], ln2_b[0], seq=S, tm=512, te=512)
    return out.reshape(B, S, D)
```
